```python
import jax, jax.numpy as jnp
from jax import lax
import numpy as np

D_MODEL = 2048
BATCH = 8
SEQ = 4096
DEPTH = 4

GRID_W = 64
CTX_LEN = 256
N_MIXERS = 3
N_LAYERS_A = (DEPTH + 2) // 3
N_LAYERS_B = (DEPTH + 1) // 3
N_LAYERS_C = DEPTH // 3
N_MOD = 6
DEEPNORM_ALPHA = (2 * DEPTH) ** 0.25
DEEPNORM_BETA = (8 * DEPTH) ** -0.25
LN_EPS = 1e-6
A_HEADS = 4
A_DV = D_MODEL // A_HEADS
A_DK = A_DV // 2
A_CHUNK = 128
GATE_CAP = 15.0
A_IN = 2 * A_HEADS * A_DK + A_HEADS * A_DV + D_MODEL + 4 * A_HEADS
B_CHUNK = 128
B_INNER = 2 * D_MODEL
B_GROUPS = 8
C_HEADS = 16
C_KV_HEADS = 4
C_HEAD_DIM = D_MODEL // C_HEADS
C_QBLOCK = 128
ROPE_THETA = 10000.0
P_HEADS = 8
P_NKEYS = 128
P_EXPERTS = P_NKEYS * P_NKEYS
P_TOPK = 16
P_DQ = 256
P_BLOCK = 128

kernel_name = 'hybrid_mlstm_gmlp_gqa_peer_dit'


def layer_norm(x, g, b):
    xf = x.astype(jnp.float32)
    mu = jnp.mean(xf, -1, keepdims=True)
    var = jnp.mean(jnp.square(xf - mu), -1, keepdims=True)
    return ((xf - mu) * lax.rsqrt(var + LN_EPS) * g.astype(jnp.float32) + b.astype(jnp.float32)).astype(x.dtype)


def rms_norm(x, g):
    xf = x.astype(jnp.float32)
    return (xf * lax.rsqrt(jnp.mean(jnp.square(xf), -1, keepdims=True) + LN_EPS) * g.astype(jnp.float32)).astype(x.dtype)


def modulate(x, shift, scale):
    return x * (1 + scale) + shift


def _mlstm_project(h, w_in, b_gate):
    bsz, n, _ = h.shape
    z = h @ w_in
    qk = A_HEADS * A_DK
    vd = A_HEADS * A_DV
    q, k, v, o, g = jnp.split(z, [qk, 2 * qk, 2 * qk + vd, 2 * qk + vd + D_MODEL], axis=-1)
    def heads(t, d):
        return t.reshape(bsz, n, A_HEADS, d).transpose(0, 2, 1, 3).astype(jnp.float32)
    q = heads(q, A_DK)
    k = heads(k, A_DK) * (A_DK ** -0.5)
    v = heads(v, A_DV)
    g = g.astype(jnp.float32) + b_gate.astype(jnp.float32)
    g = GATE_CAP * jnp.tanh(g / GATE_CAP)
    g = g.reshape(bsz, n, 4, A_HEADS).transpose(2, 0, 3, 1)
    fwd = (g[0], jax.nn.log_sigmoid(g[1]))
    bwd = (g[2], jax.nn.log_sigmoid(g[3]))
    return q, k, v, o, fwd, bwd


def mlstm_scan(q, k, v, li, lf, state):
    bsz, nh, n, _ = q.shape
    nc = n // A_CHUNK
    def chunks(t):
        return jnp.moveaxis(t.reshape(bsz, nh, nc, A_CHUNK, *t.shape[3:]), 2, 0)
    mask = jnp.tril(jnp.ones((A_CHUNK, A_CHUNK), dtype=bool))
    def body(carry, inp):
        cmat, nvec, m = carry
        qc, kc, vc, lic, lfc = inp
        b = jnp.cumsum(lfc, axis=-1)
        g = b[..., -1]
        dmat = jnp.where(mask, b[..., :, None] - b[..., None, :] + lic[..., None, :], -jnp.inf)
        inter = b + m[..., None]
        mt = jnp.maximum(inter, jnp.max(dmat, -1))
        w_intra = jnp.exp(dmat - mt[..., None])
        w_inter = jnp.exp(inter - mt)
        s = jnp.einsum('bhtd,bhsd->bhts', qc, kc) * w_intra
        num = w_inter[..., None] * jnp.einsum('bhvd,bhtd->bhtv', cmat, qc) + jnp.einsum('bhts,bhsv->bhtv', s, vc)
        den = w_inter * jnp.einsum('bhd,bhtd->bht', nvec, qc) + jnp.sum(s, -1)
        h = num / jnp.maximum(jnp.abs(den), jnp.exp(-mt))[..., None]
        wk = g[..., None] - b + lic
        m_new = jnp.maximum(g + m, jnp.max(wk, -1))
        decay = jnp.exp(g + m - m_new)
        ws = jnp.exp(wk - m_new[..., None])
        cmat = decay[..., None, None] * cmat + jnp.einsum('bhsv,bhsd->bhvd', vc * ws[..., None], kc)
        nvec = decay[..., None] * nvec + jnp.einsum('bhs,bhsd->bhd', ws, kc)
        return (cmat, nvec, m_new), h
    state, hs = lax.scan(body, state, (chunks(q), chunks(k), chunks(v), chunks(li), chunks(lf)))
    hs = jnp.moveaxis(hs, 0, 2).reshape(bsz, nh, n, -1)
    return hs, state


def _mlstm_out(hsum, o, norm_g, w_out, dtype):
    bsz, nh, n, dv = hsum.shape
    hh = jnp.transpose(hsum, (0, 2, 1, 3))
    hh = hh * lax.rsqrt(jnp.mean(jnp.square(hh), -1, keepdims=True) + LN_EPS)
    hh = hh.reshape(bsz, n, nh * dv) * norm_g.astype(jnp.float32) * jax.nn.sigmoid(o.astype(jnp.float32))
    return hh.astype(dtype) @ w_out


def mlstm_mixer(h_lat, h_ctx, w_in, b_gate, norm_g, w_out, need_ctx):
    ql, kl, vl, ol, gfl, gbl = _mlstm_project(h_lat, w_in, b_gate)
    qc, kc, vc, oc, gfc, gbc = _mlstm_project(h_ctx, w_in, b_gate)
    bsz = h_lat.shape[0]
    zero = (jnp.zeros((bsz, A_HEADS, A_DV, A_DK), jnp.float32),
            jnp.zeros((bsz, A_HEADS, A_DK), jnp.float32),
            jnp.zeros((bsz, A_HEADS), jnp.float32))
    def flip(t):
        return jnp.flip(t, axis=2)
    hcf, st_f = mlstm_scan(qc, kc, vc, gfc[0], gfc[1], zero)
    hcb, st_b = mlstm_scan(flip(qc), flip(kc), flip(vc), flip(gbc[0]), flip(gbc[1]), zero)
    hlf, _ = mlstm_scan(ql, kl, vl, gfl[0], gfl[1], st_f)
    hlb, _ = mlstm_scan(flip(ql), flip(kl), flip(vl), flip(gbl[0]), flip(gbl[1]), st_b)
    y_lat = _mlstm_out(hlf + flip(hlb), ol, norm_g, w_out, h_lat.dtype)
    y_ctx = _mlstm_out(hcf + flip(hcb), oc, norm_g, w_out, h_ctx.dtype) if need_ctx else None
    return y_lat, y_ctx


def chunk_mlp_mixer(h_lat, h_ctx, w_in, b_in, ln_g_v, ln_b_v, w_s, b_s, w_out, need_ctx):
    def apply(h):
        bsz, n, _ = h.shape
        z = jax.nn.gelu(h @ w_in + b_in)
        u, v = z[..., :B_INNER], z[..., B_INNER:]
        v = layer_norm(v, ln_g_v, ln_b_v).reshape(bsz, n // B_CHUNK, B_CHUNK, B_GROUPS, B_INNER // B_GROUPS)
        sv = jnp.einsum('gts,bcsgd->bctgd', w_s, v) + b_s.T[:, :, None]
        return (u * sv.reshape(bsz, n, B_INNER)) @ w_out
    return apply(h_lat), (apply(h_ctx) if need_ctx else None)


def rope_1d(x, pos):
    half = x.shape[-1] // 2
    freqs = ROPE_THETA ** (-jnp.arange(half, dtype=jnp.float32) / half)
    ang = pos.astype(jnp.float32)[:, None] * freqs
    cos = jnp.cos(ang)[:, None, :]
    sin = jnp.sin(ang)[:, None, :]
    x1, x2 = x[..., :half], x[..., half:]
    return jnp.concatenate([x1 * cos - x2 * sin, x1 * sin + x2 * cos], -1)


def axial_rope(x, rows, cols):
    xf = x.astype(jnp.float32)
    half = x.shape[-1] // 2
    return jnp.concatenate([rope_1d(xf[..., :half], rows), rope_1d(xf[..., half:], cols)], -1).astype(x.dtype)


def gqa_mixer(h_lat, h_ctx, w_qkv, q_g, k_g, w_out, rows, cols, need_ctx):
    n_grp = C_HEADS // C_KV_HEADS
    scale = C_HEAD_DIM ** -0.5
    def project(h):
        bsz, n, _ = h.shape
        q, k, v = jnp.split(h @ w_qkv, [C_HEADS * C_HEAD_DIM, (C_HEADS + C_KV_HEADS) * C_HEAD_DIM], axis=-1)
        q = rms_norm(q.reshape(bsz, n, C_HEADS, C_HEAD_DIM), q_g)
        k = rms_norm(k.reshape(bsz, n, C_KV_HEADS, C_HEAD_DIM), k_g)
        return q, k, v.reshape(bsz, n, C_KV_HEADS, C_HEAD_DIM)
    def attend(q, k, v):
        s = jnp.einsum('bkgqd,bskd->bkgqs', q, k).astype(jnp.float32) * scale
        p = jax.nn.softmax(s, axis=-1).astype(v.dtype)
        return jnp.einsum('bkgqs,bskd->bqkgd', p, v)
    ql, kl, vl = project(h_lat)
    qc, kc, vc = project(h_ctx)
    ql = axial_rope(ql, rows, cols)
    kl = axial_rope(kl, rows, cols)
    k_all = jnp.concatenate([kl, kc], axis=1)
    v_all = jnp.concatenate([vl, vc], axis=1)
    bsz, n = h_lat.shape[:2]
    qb = ql.reshape(bsz, n // C_QBLOCK, C_QBLOCK, C_KV_HEADS, n_grp, C_HEAD_DIM).transpose(1, 0, 3, 4, 2, 5)
    ob = lax.map(lambda qblk: attend(qblk, k_all, v_all), qb)
    y_lat = ob.transpose(1, 0, 2, 3, 4, 5).reshape(bsz, n, D_MODEL) @ w_out
    y_ctx = None
    if need_ctx:
        n_c = h_ctx.shape[1]
        qcr = qc.reshape(bsz, n_c, C_KV_HEADS, n_grp, C_HEAD_DIM).transpose(0, 2, 3, 1, 4)
        y_ctx = attend(qcr, kc, vc).reshape(bsz, n_c, D_MODEL) @ w_out
    return y_lat, y_ctx


def peer_ffn(h, w_q, k1, k2, w_u, w_v):
    shp = h.shape
    tok = h.reshape(-1, P_BLOCK, shp[-1])
    half = P_DQ // 2
    def block(hb):
        q = (hb @ w_q).reshape(P_BLOCK, P_HEADS, P_DQ)
        s1 = jnp.einsum('thd,kd->thk', q[..., :half], k1).astype(jnp.float32)
        s2 = jnp.einsum('thd,kd->thk', q[..., half:], k2).astype(jnp.float32)
        v1, i1 = lax.top_k(s1, P_TOPK)
        v2, i2 = lax.top_k(s2, P_TOPK)
        cand = (v1[..., :, None] + v2[..., None, :]).reshape(P_BLOCK, P_HEADS, P_TOPK * P_TOPK)
        sc, ci = lax.top_k(cand, P_TOPK)
        e = jnp.take_along_axis(i1, ci // P_TOPK, -1) * P_NKEYS + jnp.take_along_axis(i2, ci % P_TOPK, -1)
        gate = jax.nn.softmax(sc, axis=-1)
        u = jnp.take(w_u, e, axis=0)
        a = jnp.einsum('td,thkd->thk', hb, u)
        hid = (jax.nn.gelu(a.astype(jnp.float32)) * gate).astype(hb.dtype)
        vv = jnp.take(w_v, e, axis=0)
        return jnp.einsum('thk,thkd->td', hid, vv)
    return lax.map(block, tok).reshape(shp)


def setup_inputs(seed: int = 0) -> dict:
    key = jax.random.key(seed)
    ks = iter(jax.random.split(key, 40))
    def nrm(shape, scale):
        return jax.random.normal(next(ks), shape, jnp.float32) * scale
    D = D_MODEL
    f_bias = jnp.linspace(3.0, 6.0, A_HEADS, dtype=jnp.float32)
    zeros_h = jnp.zeros((A_HEADS,), jnp.float32)
    gate_base = jnp.stack([zeros_h, f_bias, zeros_h, f_bias])
    return {
        'x': nrm((BATCH, SEQ, D), 1.0),
        'c': nrm((BATCH, D), 1.0),
        'ctx': nrm((BATCH, CTX_LEN, D), 1.0),
        'c_ctx': nrm((D,), 1.0),
        'w_mod': nrm((DEPTH, D, N_MOD * D), 0.5 * D ** -0.5),
        'b_mod': nrm((DEPTH, N_MOD * D), 0.02),
        'ln_g': 1.0 + nrm((DEPTH, 2, D), 0.02),
        'ln_b': nrm((DEPTH, 2, D), 0.02),
        'a_w_in': nrm((N_LAYERS_A, D, A_IN), D ** -0.5),
        'a_b_gate': (gate_base[None] + nrm((N_LAYERS_A, 4, A_HEADS), 0.1)).reshape(N_LAYERS_A, 4 * A_HEADS),
        'a_norm_g': 1.0 + nrm((N_LAYERS_A, A_HEADS * A_DV), 0.02),
        'a_w_out': nrm((N_LAYERS_A, A_HEADS * A_DV, D), (A_HEADS * A_DV) ** -0.5 * DEEPNORM_BETA),
        'b_w_in': nrm((N_LAYERS_B, D, 2 * B_INNER), D ** -0.5),
        'b_b_in': nrm((N_LAYERS_B, 2 * B_INNER), 0.02),
        'b_ln_g': 1.0 + nrm((N_LAYERS_B, B_INNER), 0.02),
        'b_ln_b': nrm((N_LAYERS_B, B_INNER), 0.02),
        'b_w_s': nrm((N_LAYERS_B, B_GROUPS, B_CHUNK, B_CHUNK), B_CHUNK ** -0.5),
        'b_b_s': 1.0 + nrm((N_LAYERS_B, B_GROUPS, B_CHUNK), 0.02),
        'b_w_out': nrm((N_LAYERS_B, B_INNER, D), B_INNER ** -0.5 * DEEPNORM_BETA),
        'c_w_qkv': nrm((N_LAYERS_C, D, (C_HEADS + 2 * C_KV_HEADS) * C_HEAD_DIM), D ** -0.5),
        'c_q_g': 1.0 + nrm((N_LAYERS_C, C_HEAD_DIM), 0.02),
        'c_k_g': 1.0 + nrm((N_LAYERS_C, C_HEAD_DIM), 0.02),
        'c_w_out': nrm((N_LAYERS_C, C_HEADS * C_HEAD_DIM, D), (C_HEADS * C_HEAD_DIM) ** -0.5 * DEEPNORM_BETA),
        'p_w_q': nrm((DEPTH, D, P_HEADS * P_DQ), D ** -0.5),
        'p_k1': nrm((DEPTH, P_NKEYS, P_DQ // 2), (P_DQ // 2) ** -0.5),
        'p_k2': nrm((DEPTH, P_NKEYS, P_DQ // 2), (P_DQ // 2) ** -0.5),
        'p_u': nrm((DEPTH, P_EXPERTS, D), D ** -0.5),
        'p_v': nrm((DEPTH, P_EXPERTS, D), DEEPNORM_BETA),
    }


def reference(x, c, ctx, c_ctx, w_mod, b_mod, ln_g, ln_b, a_w_in, a_b_gate, a_norm_g, a_w_out,
              b_w_in, b_b_in, b_ln_g, b_ln_b, b_w_s, b_b_s, b_w_out, c_w_qkv, c_q_g, c_k_g, c_w_out,
              p_w_q, p_k1, p_k2, p_u, p_v):
    bsz, n, _ = x.shape
    grid_rows = n // GRID_W
    rows = jnp.repeat(jnp.arange(grid_rows, dtype=jnp.int32), GRID_W)
    cols = jnp.tile(jnp.arange(GRID_W, dtype=jnp.int32), grid_rows)
    cond_lat = jax.nn.silu(c)
    cond_ctx = jax.nn.silu(c_ctx)
    x_lat, x_ctx = x, ctx
    for i in range(DEPTH):
        need_ctx = i < DEPTH - 1
        mixer, j = i % N_MIXERS, i // N_MIXERS
        mod_l = (cond_lat @ w_mod[i] + b_mod[i]).reshape(bsz, N_MOD, 1, D_MODEL)
        mod_c = (cond_ctx @ w_mod[i] + b_mod[i]).reshape(N_MOD, D_MODEL)
        h_l = modulate(x_lat, mod_l[:, 0], mod_l[:, 1])
        h_c = modulate(x_ctx, mod_c[0], mod_c[1])
        if mixer == 0:
            y_l, y_c = mlstm_mixer(h_l, h_c, a_w_in[j], a_b_gate[j], a_norm_g[j], a_w_out[j], need_ctx)
        elif mixer == 1:
            y_l, y_c = chunk_mlp_mixer(h_l, h_c, b_w_in[j], b_b_in[j], b_ln_g[j], b_ln_b[j], b_w_s[j], b_b_s[j],
                                       b_w_out[j], need_ctx)
        else:
            y_l, y_c = gqa_mixer(h_l, h_c, c_w_qkv[j], c_q_g[j], c_k_g[j], c_w_out[j], rows, cols, need_ctx)
        x_lat = layer_norm(DEEPNORM_ALPHA * x_lat + mod_l[:, 2] * y_l, ln_g[i, 0], ln_b[i, 0])
        if need_ctx:
            x_ctx = layer_norm(DEEPNORM_ALPHA * x_ctx + mod_c[2] * y_c, ln_g[i, 0], ln_b[i, 0])
        h_l = modulate(x_lat, mod_l[:, 3], mod_l[:, 4])
        f_l = peer_ffn(h_l, p_w_q[i], p_k1[i], p_k2[i], p_u[i], p_v[i])
        x_lat = layer_norm(DEEPNORM_ALPHA * x_lat + mod_l[:, 5] * f_l, ln_g[i, 1], ln_b[i, 1])
        if need_ctx:
            h_c = modulate(x_ctx, mod_c[3], mod_c[4])
            f_c = peer_ffn(h_c, p_w_q[i], p_k1[i], p_k2[i], p_u[i], p_v[i])
            x_ctx = layer_norm(DEEPNORM_ALPHA * x_ctx + mod_c[5] * f_c, ln_g[i, 1], ln_b[i, 1])
    return x_lat
```

```python
import functools
import math

import jax
import jax.numpy as jnp
from jax import lax
from jax.experimental import pallas as pl
from jax.experimental.pallas import tpu as pltpu

F32 = jnp.float32
BF16 = jnp.bfloat16

LN_EPS = 1e-6
GRID_W = 64
N_MOD = 6
A_HEADS = 4
A_CHUNK = 128
GATE_CAP = 15.0
B_CHUNK = 128
B_GROUPS = 8
C_HEADS = 16
C_KV_HEADS = 4
C_HEAD_DIM = 128
ROPE_THETA = 10000.0
P_HEADS = 8
P_NKEYS = 128
P_TOPK = 16

VMEM_LIMIT = 56 * 1024 * 1024
_NT = (((1,), (1,)), ((), ()))


def _cparams(*sem):
    return pltpu.CompilerParams(dimension_semantics=sem, vmem_limit_bytes=VMEM_LIMIT)


def _row_tile(n_lat_per_batch, n_rows, pref):
    t = pref
    while t > 8 and (n_lat_per_batch % t or n_rows % t):
        t //= 2
    return t


def _gelu(y):
    return 0.5 * y * (1.0 + jnp.tanh(0.7978845608028654 * (y + 0.044715 * (y * y * y))))


def _layer_norm_rows(r, g, b):
    mu = jnp.mean(r, axis=-1, keepdims=True)
    d = r - mu
    var = jnp.mean(d * d, axis=-1, keepdims=True)
    return d * lax.rsqrt(var + LN_EPS) * g + b


def _mod_index(tm, s_lat, n_batch):
    def idx(i, *_):
        return (jnp.minimum((i * tm) // s_lat, n_batch), 0, 0)
    return idx


def _mod_kernel(c_ref, w_ref, b_ref, o_ref):
    c = c_ref[...]
    s = c / (1.0 + jnp.exp(-c))
    o_ref[...] = jnp.dot(s, w_ref[...], preferred_element_type=F32,
                         precision=lax.Precision.HIGHEST) + b_ref[...]


def _modulation(c_all, w_mod, b_mod):
    depth, d, n = w_mod.shape
    rows = c_all.shape[0]
    tn = 1024
    return pl.pallas_call(
        _mod_kernel,
        grid=(depth, n // tn),
        in_specs=[pl.BlockSpec((rows, d), lambda l, j: (0, 0)),
                  pl.BlockSpec((None, d, tn), lambda l, j: (l, 0, j)),
                  pl.BlockSpec((None, 1, tn), lambda l, j: (l, 0, j))],
        out_specs=pl.BlockSpec((None, rows, tn), lambda l, j: (l, 0, j)),
        out_shape=jax.ShapeDtypeStruct((depth, rows, n), F32),
        compiler_params=_cparams("parallel", "parallel"),
        name="adaln_mod",
    )(c_all, w_mod, b_mod.reshape(depth, 1, n))


def _linear_kernel(x_ref, mod_ref, w_ref, b_ref, o_ref, h_ref, *, shift_row, act):
    @pl.when(pl.program_id(1) == 0)
    def _():
        h = x_ref[...] * (1.0 + mod_ref[shift_row + 1:shift_row + 2, :]) + mod_ref[shift_row:shift_row + 1, :]
        h_ref[...] = h.astype(BF16)

    y = jnp.dot(h_ref[...], w_ref[...], preferred_element_type=F32) + b_ref[...]
    if act == "gelu":
        y = _gelu(y)
    o_ref[...] = y.astype(o_ref.dtype)


def _mod_linear(x, mod, w, b, *, shift_row, act, s_lat, n_batch, tm, tn, out_dtype=BF16, name):
    r, d = x.shape
    n = w.shape[1]
    return pl.pallas_call(
        functools.partial(_linear_kernel, shift_row=shift_row, act=act),
        grid=(r // tm, n // tn),
        in_specs=[pl.BlockSpec((tm, d), lambda i, j: (i, 0)),
                  pl.BlockSpec((None, 8, d), _mod_index(tm, s_lat, n_batch)),
                  pl.BlockSpec((d, tn), lambda i, j: (0, j)),
                  pl.BlockSpec((1, tn), lambda i, j: (0, j))],
        out_specs=pl.BlockSpec((tm, tn), lambda i, j: (i, j)),
        out_shape=jax.ShapeDtypeStruct((r, n), out_dtype),
        scratch_shapes=[pltpu.VMEM((tm, d), BF16)],
        compiler_params=_cparams("parallel", "arbitrary"),
        name=name,
    )(x, mod, w, b)


def _out_ln_kernel(*refs, a_fn, n_a, gate_row, alpha):
    a_refs = refs[:n_a]
    w_ref, x_ref, mod_ref, g_ref, b_ref, o_ref = refs[n_a:]
    a = a_fn(*a_refs)
    y = jnp.dot(a, w_ref[...], preferred_element_type=F32)
    r = alpha * x_ref[...] + mod_ref[gate_row:gate_row + 1, :] * y
    o_ref[...] = _layer_norm_rows(r, g_ref[...], b_ref[...])


def _out_ln(a_inputs, a_specs, a_fn, w, x, mod, ln_g, ln_b, *, gate_row, alpha, s_lat, n_batch, tm, name):
    r, d = x.shape
    k = w.shape[0]
    return pl.pallas_call(
        functools.partial(_out_ln_kernel, a_fn=a_fn, n_a=len(a_inputs), gate_row=gate_row, alpha=alpha),
        grid=(r // tm,),
        in_specs=list(a_specs) + [
            pl.BlockSpec((k, d), lambda i: (0, 0)),
            pl.BlockSpec((tm, d), lambda i: (i, 0)),
            pl.BlockSpec((None, 8, d), _mod_index(tm, s_lat, n_batch)),
            pl.BlockSpec((1, d), lambda i: (0, 0)),
            pl.BlockSpec((1, d), lambda i: (0, 0))],
        out_specs=pl.BlockSpec((tm, d), lambda i: (i, 0)),
        out_shape=jax.ShapeDtypeStruct((r, d), F32),
        compiler_params=_cparams("parallel"),
        name=name,
    )(*a_inputs, w, x, mod, ln_g.reshape(1, d), ln_b.reshape(1, d))


def _gate_prep_kernel(x_ref, mod_ref, wi_ref, wf_ref, bi_ref, bf_ref, r_ref, b_ref, *, n_chunks):
    h = x_ref[...] * (1.0 + mod_ref[1:2, :]) + mod_ref[0:1, :]
    hp = lax.Precision.HIGHEST
    gi = jnp.dot(h, wi_ref[...], preferred_element_type=F32, precision=hp) + bi_ref[...]
    gf = jnp.dot(h, wf_ref[...], preferred_element_type=F32, precision=hp) + bf_ref[...]
    gi = GATE_CAP * jnp.tanh(gi / GATE_CAP)
    gf = GATE_CAP * jnp.tanh(gf / GATE_CAP)
    lf = jnp.minimum(gf, 0.0) - jnp.log(1.0 + jnp.exp(-jnp.abs(gf)))
    ti = lax.broadcasted_iota(jnp.int32, (A_CHUNK, A_CHUNK), 0)
    si = lax.broadcasted_iota(jnp.int32, (A_CHUNK, A_CHUNK), 1)
    tri_lo = (si <= ti).astype(F32)
    tri_up = (si >= ti).astype(F32)
    lane = lax.broadcasted_iota(jnp.int32, (A_CHUNK, 128), 1)
    for c in range(n_chunks):
        sl = slice(c * A_CHUNK, (c + 1) * A_CHUNK)
        lfc = lf[sl]
        cum_f = jnp.dot(tri_lo, lfc, preferred_element_type=F32, precision=hp)
        cum_b = jnp.dot(tri_up, lfc, preferred_element_type=F32, precision=hp)
        bcum = jnp.where(lane < A_HEADS, cum_f, cum_b)
        b_ref[sl, :] = bcum
        r_ref[sl, :] = gi[sl] - bcum


def _mlstm_gates(x, mod, w_gate, b_gate, *, s_lat, n_batch, tm):
    r, d = x.shape
    nh = A_HEADS
    wg = w_gate.reshape(d, 4, nh)
    bg = b_gate.reshape(4, nh)
    pad = 128 - 2 * nh
    wi = jnp.pad(jnp.concatenate([wg[:, 0], wg[:, 2]], axis=1), ((0, 0), (0, pad)))
    wf = jnp.pad(jnp.concatenate([wg[:, 1], wg[:, 3]], axis=1), ((0, 0), (0, pad)))
    bi = jnp.pad(jnp.concatenate([bg[0], bg[2]]), (0, pad)).reshape(1, 128)
    bf = jnp.pad(jnp.concatenate([bg[1], bg[3]]), (0, pad)).reshape(1, 128)
    row = pl.BlockSpec((tm, 128), lambda i: (i, 0))
    full = lambda shape: pl.BlockSpec(shape, lambda i: (0, 0))
    return pl.pallas_call(
        functools.partial(_gate_prep_kernel, n_chunks=tm // A_CHUNK),
        grid=(r // tm,),
        in_specs=[pl.BlockSpec((tm, d), lambda i: (i, 0)),
                  pl.BlockSpec((None, 8, d), _mod_index(tm, s_lat, n_batch)),
                  full((d, 128)), full((d, 128)), full((1, 128)), full((1, 128))],
        out_specs=[row, row],
        out_shape=[jax.ShapeDtypeStruct((r, 128), F32)] * 2,
        compiler_params=_cparams("parallel"),
        name="mlstm_gates",
    )(x, mod, wi, wf, bi, bf)


def _mlstm_scan_kernel(q_ref, k_ref, v_ref, rcol_ref, bcol_ref, rrow_ref, o_ref,
                       ct_ref, n_ref, m_ref, *, dk, dv):
    d = pl.program_id(1)
    c = pl.program_id(2)

    @pl.when(c == 0)
    def _():
        ct_ref[...] = jnp.zeros_like(ct_ref)
        n_ref[...] = jnp.zeros_like(n_ref)
        m_ref[...] = jnp.zeros_like(m_ref)

    L = A_CHUNK
    ti = lax.broadcasted_iota(jnp.int32, (L, L), 0)
    si = lax.broadcasted_iota(jnp.int32, (L, L), 1)
    fwd = d == 0
    mask = (si - ti) * (1 - 2 * d) <= 0
    kscale = dk ** -0.5
    for h in range(A_HEADS):
        q = q_ref[:, h * dk:(h + 1) * dk]
        k32 = k_ref[:, h * dk:(h + 1) * dk].astype(F32) * kscale
        k = k32.astype(BF16)
        v = v_ref[:, h * dv:(h + 1) * dv]
        r_col = jnp.where(fwd, rcol_ref[:, h:h + 1], rcol_ref[:, A_HEADS + h:A_HEADS + h + 1])
        b_col = jnp.where(fwd, bcol_ref[:, h:h + 1], bcol_ref[:, A_HEADS + h:A_HEADS + h + 1])
        r_row = rrow_ref[pl.ds(d * A_HEADS + h, 1), :]
        m_prev = m_ref[h:h + 1, 0:1]
        n_row = n_ref[h:h + 1, :]

        dlog = jnp.where(mask, r_row, -jnp.inf)
        big_m = jnp.maximum(jnp.max(dlog, axis=1, keepdims=True), m_prev)
        w_intra = jnp.exp(dlog - big_m)
        w_inter = jnp.exp(m_prev - big_m)
        s = lax.dot_general(q, k, _NT, preferred_element_type=F32) * w_intra
        inter = jnp.dot(q, ct_ref[h].astype(BF16), preferred_element_type=F32)
        num = w_inter * inter + jnp.dot(s.astype(BF16), v, preferred_element_type=F32)
        qn = jnp.sum(q.astype(F32) * n_row, axis=1, keepdims=True)
        den = w_inter * qn + jnp.sum(s, axis=1, keepdims=True)
        hout = num / jnp.maximum(jnp.abs(den), jnp.exp(-(b_col + big_m)))
        o_ref[:, h * dv:(h + 1) * dv] = hout.astype(o_ref.dtype)

        m_last = jnp.maximum(jnp.max(r_col, axis=0, keepdims=True), m_prev)
        g_tot = jnp.min(b_col, axis=0, keepdims=True)
        decay = jnp.exp(m_prev - m_last)
        ws = jnp.exp(r_col - m_last)
        vw = (v.astype(F32) * ws).astype(BF16)
        kt = k32.T.astype(BF16)
        ct_ref[h] = decay * ct_ref[h] + jnp.dot(kt, vw, preferred_element_type=F32)
        n_ref[h:h + 1, :] = decay * n_row + jnp.sum(k32 * ws, axis=0, keepdims=True)
        m_ref[h:h + 1, :] = jnp.broadcast_to(g_tot + m_last, (1, m_ref.shape[1]))


def _mlstm_scan(z, r_col, b_col, r_row, *, n_batch, s_lat, n_ctx, dk, dv):
    r = z.shape[0]
    L = A_CHUNK
    ncl, ncc = s_lat // L, n_ctx // L
    ctx0 = n_batch * ncl

    def chunk(b, d, c):
        pos_c = jnp.where(d == 0, c, ncc - 1 - c)
        pos_l = jnp.where(d == 0, c - ncc, ncl - 1 - (c - ncc))
        return jnp.where(c < ncc, ctx0 + b * ncc + pos_c, b * ncl + pos_l)

    hq = A_HEADS * dk
    hv = A_HEADS * dv
    return pl.pallas_call(
        functools.partial(_mlstm_scan_kernel, dk=dk, dv=dv),
        grid=(n_batch, 2, ncc + ncl),
        in_specs=[pl.BlockSpec((L, hq), lambda b, d, c: (chunk(b, d, c), 0)),
                  pl.BlockSpec((L, hq), lambda b, d, c: (chunk(b, d, c), 1)),
                  pl.BlockSpec((L, hv), lambda b, d, c: (chunk(b, d, c), (2 * hq) // hv)),
                  pl.BlockSpec((L, 128), lambda b, d, c: (chunk(b, d, c), 0)),
                  pl.BlockSpec((L, 128), lambda b, d, c: (chunk(b, d, c), 0)),
                  pl.BlockSpec((2 * A_HEADS, L), lambda b, d, c: (0, chunk(b, d, c)))],
        out_specs=pl.BlockSpec((None, L, hv), lambda b, d, c: (d, chunk(b, d, c), 0)),
        out_shape=jax.ShapeDtypeStruct((2, r, hv), BF16),
        scratch_shapes=[pltpu.VMEM((A_HEADS, dk, dv), F32),
                        pltpu.VMEM((8, dk), F32),
                        pltpu.VMEM((8, 128), F32)],
        compiler_params=_cparams("parallel", "parallel", "arbitrary"),
        name="mlstm_scan",
    )(z, z, z, r_col, b_col, r_row)


def _mlstm_out_a(hf_ref, hb_ref, o_ref, g_ref, *, dv):
    hs = hf_ref[...].astype(F32) + hb_ref[...].astype(F32)
    og = o_ref[...].astype(F32)
    og = 1.0 / (1.0 + jnp.exp(-og))
    parts = []
    for h in range(A_HEADS):
        x = hs[:, h * dv:(h + 1) * dv]
        parts.append(x * lax.rsqrt(jnp.mean(x * x, axis=-1, keepdims=True) + LN_EPS))
    hn = jnp.concatenate(parts, axis=1)
    return (hn * g_ref[...] * og).astype(BF16)


def _mlstm_mixer(x, mod, ln_g, ln_b, w_in, b_gate, norm_g, w_out, *, alpha, s_lat, n_ctx, n_batch):
    r, d = x.shape
    dv = d // A_HEADS
    dk = dv // 2
    n_main = 2 * A_HEADS * dk + A_HEADS * dv + d
    tm = _row_tile(s_lat, r, 512)
    w_main = w_in[:, :n_main].astype(BF16)
    z = _mod_linear(x, mod, w_main, jnp.zeros((1, n_main), F32), shift_row=0, act=None,
                    s_lat=s_lat, n_batch=n_batch, tm=tm, tn=1024, name="mlstm_in")
    r_col, b_col = _mlstm_gates(x, mod, w_in[:, n_main:], b_gate, s_lat=s_lat, n_batch=n_batch, tm=tm)
    r_row = r_col[:, :2 * A_HEADS].T
    hdir = _mlstm_scan(z, r_col, b_col, r_row, n_batch=n_batch, s_lat=s_lat, n_ctx=n_ctx, dk=dk, dv=dv)
    hv = A_HEADS * dv
    a_specs = [pl.BlockSpec((None, tm, hv), lambda i: (0, i, 0)),
               pl.BlockSpec((None, tm, hv), lambda i: (1, i, 0)),
               pl.BlockSpec((tm, d), lambda i: (i, (n_main - d) // d)),
               pl.BlockSpec((1, hv), lambda i: (0, 0))]
    return _out_ln([hdir, hdir, z, norm_g.reshape(1, hv)], a_specs, functools.partial(_mlstm_out_a, dv=dv),
                   w_out.astype(BF16), x, mod, ln_g, ln_b, gate_row=2, alpha=alpha,
                   s_lat=s_lat, n_batch=n_batch, tm=tm, name="mlstm_out")


def _gmlp_gate_kernel(z_ref, g_ref, b_ref, ws_ref, bs_ref, o_ref, *, n_chunks, inner):
    gd = inner // B_GROUPS
    for c in range(n_chunks):
        sl = slice(c * B_CHUNK, (c + 1) * B_CHUNK)
        v = z_ref[sl, inner:].astype(F32)
        vn = _layer_norm_rows(v, g_ref[...], b_ref[...]).astype(BF16)
        for g in range(B_GROUPS):
            cs = slice(g * gd, (g + 1) * gd)
            sv = jnp.dot(ws_ref[g], vn[:, cs], preferred_element_type=F32) + bs_ref[:, g:g + 1]
            o_ref[sl, cs] = (z_ref[sl, cs].astype(F32) * sv).astype(o_ref.dtype)


def _gmlp_mixer(x, mod, ln_g, ln_b, w_in, b_in, g_v, b_v, w_s, b_s, w_out, *, alpha, s_lat, n_batch):
    r, d = x.shape
    inner = w_out.shape[0]
    tm = _row_tile(s_lat, r, 512)
    z = _mod_linear(x, mod, w_in.astype(BF16), b_in.reshape(1, -1), shift_row=0, act="gelu",
                    s_lat=s_lat, n_batch=n_batch, tm=tm, tn=1024, name="gmlp_in")
    tg = _row_tile(s_lat, r, 256)
    gated = pl.pallas_call(
        functools.partial(_gmlp_gate_kernel, n_chunks=tg // B_CHUNK, inner=inner),
        grid=(r // tg,),
        in_specs=[pl.BlockSpec((tg, 2 * inner), lambda i: (i, 0)),
                  pl.BlockSpec((1, inner), lambda i: (0, 0)),
                  pl.BlockSpec((1, inner), lambda i: (0, 0)),
                  pl.BlockSpec((B_GROUPS, B_CHUNK, B_CHUNK), lambda i: (0, 0, 0)),
                  pl.BlockSpec((B_CHUNK, B_GROUPS), lambda i: (0, 0))],
        out_specs=pl.BlockSpec((tg, inner), lambda i: (i, 0)),
        out_shape=jax.ShapeDtypeStruct((r, inner), BF16),
        compiler_params=_cparams("parallel"),
        name="gmlp_gate",
    )(z, g_v.reshape(1, inner), b_v.reshape(1, inner), w_s.astype(BF16), b_s.T)
    a_specs = [pl.BlockSpec((tm, inner), lambda i: (i, 0))]
    return _out_ln([gated], a_specs, lambda a_ref: a_ref[...], w_out.astype(BF16), x, mod, ln_g, ln_b,
                   gate_row=2, alpha=alpha, s_lat=s_lat, n_batch=n_batch, tm=tm, name="gmlp_out")


def _qk_prep_kernel(qkv_ref, cos_ref, sin_ref, qg_ref, kg_ref, q_ref, k_ref, *, n_q, n_kv, scale):
    hd = C_HEAD_DIM
    cos = cos_ref[...]
    sin = sin_ref[...]
    lane = lax.broadcasted_iota(jnp.int32, cos.shape, 1)
    first = (lane % (hd // 2)) < (hd // 4)

    def norm_rope(x, g):
        xn = x * lax.rsqrt(jnp.mean(x * x, axis=-1, keepdims=True) + LN_EPS) * g
        swapped = jnp.where(first, pltpu.roll(xn, hd - hd // 4, axis=1), pltpu.roll(xn, hd // 4, axis=1))
        return xn * cos + swapped * sin

    for h in range(n_q):
        x = qkv_ref[:, h * hd:(h + 1) * hd].astype(F32)
        q_ref[:, h * hd:(h + 1) * hd] = (norm_rope(x, qg_ref[...]) * scale).astype(q_ref.dtype)
    for h in range(n_kv):
        x = qkv_ref[:, (n_q + h) * hd:(n_q + h + 1) * hd].astype(F32)
        k_ref[:, h * hd:(h + 1) * hd] = norm_rope(x, kg_ref[...]).astype(k_ref.dtype)


def _flash_kernel(q_ref, *refs, n_kv_refs, n_grp, bk_max):
    kv_refs = refs[:2 * n_kv_refs]
    o_ref, qs_ref, m_ref, l_ref, acc_ref = refs[2 * n_kv_refs:]
    hd = C_HEAD_DIM
    bq = q_ref.shape[0]
    for g in range(n_grp):
        qs_ref[g * bq:(g + 1) * bq, :] = q_ref[:, g * hd:(g + 1) * hd]
    m_ref[...] = jnp.full_like(m_ref, -jnp.inf)
    l_ref[...] = jnp.zeros_like(l_ref)
    acc_ref[...] = jnp.zeros_like(acc_ref)

    def update(kb, vb):
        s = lax.dot_general(qs_ref[...], kb, _NT, preferred_element_type=F32)
        m = m_ref[...]
        m_new = jnp.maximum(m, jnp.max(s, axis=1, keepdims=True))
        p = jnp.exp(s - m_new)
        a = jnp.exp(m - m_new)
        l_ref[...] = a * l_ref[...] + jnp.sum(p, axis=1, keepdims=True)
        acc_ref[...] = a * acc_ref[...] + jnp.dot(p.astype(BF16), vb, preferred_element_type=F32)
        m_ref[...] = m_new

    for i in range(n_kv_refs):
        k_ref, v_ref = kv_refs[2 * i], kv_refs[2 * i + 1]
        bk = min(bk_max, k_ref.shape[0])
        n_blk = k_ref.shape[0] // bk
        if n_blk == 1:
            update(k_ref[...], v_ref[...])
        else:
            def body(j, c, k_ref=k_ref, v_ref=v_ref):
                off = pl.multiple_of(j * bk, bk)
                update(k_ref[pl.ds(off, bk), :], v_ref[pl.ds(off, bk), :])
                return c
            lax.fori_loop(0, n_blk, body, 0)
    out = acc_ref[...] / l_ref[...]
    for g in range(n_grp):
        o_ref[:, g * hd:(g + 1) * hd] = out[g * bq:(g + 1) * bq].astype(o_ref.dtype)


def _gqa_mixer(x, mod, ln_g, ln_b, w_qkv, q_g, k_g, w_out, *, alpha, s_lat, n_ctx, n_batch):
    r, d = x.shape
    hd = C_HEAD_DIM
    n_q, n_kv = C_HEADS, C_KV_HEADS
    n_grp = n_q // n_kv
    n_qkv = (n_q + 2 * n_kv) * hd
    tm = _row_tile(s_lat, r, 512)
    qkv = _mod_linear(x, mod, w_qkv.astype(BF16), jnp.zeros((1, n_qkv), F32), shift_row=0, act=None,
                      s_lat=s_lat, n_batch=n_batch, tm=tm, tn=1024, name="gqa_qkv")

    tp = _row_tile(s_lat, r, 256)
    pos = jnp.arange(s_lat, dtype=jnp.int32)
    quarter = hd // 4
    freqs = ROPE_THETA ** (-jnp.arange(quarter, dtype=F32) / quarter)
    ang_r = (pos // GRID_W).astype(F32)[:, None] * freqs
    ang_c = (pos % GRID_W).astype(F32)[:, None] * freqs
    cos_t = jnp.concatenate([jnp.cos(ang_r)] * 2 + [jnp.cos(ang_c)] * 2, axis=1)
    sin_t = jnp.concatenate([-jnp.sin(ang_r), jnp.sin(ang_r), -jnp.sin(ang_c), jnp.sin(ang_c)], axis=1)
    cos_t = jnp.concatenate([cos_t, jnp.ones((tp, hd), F32)], axis=0)
    sin_t = jnp.concatenate([sin_t, jnp.zeros((tp, hd), F32)], axis=0)
    n_lat_tiles = (n_batch * s_lat) // tp
    per_b = s_lat // tp
    tab = lambda i: (jnp.where(i < n_lat_tiles, i % per_b, per_b), 0)
    qn, kn = pl.pallas_call(
        functools.partial(_qk_prep_kernel, n_q=n_q, n_kv=n_kv, scale=hd ** -0.5),
        grid=(r // tp,),
        in_specs=[pl.BlockSpec((tp, (n_q + n_kv) * hd), lambda i: (i, 0)),
                  pl.BlockSpec((tp, hd), tab), pl.BlockSpec((tp, hd), tab),
                  pl.BlockSpec((1, hd), lambda i: (0, 0)), pl.BlockSpec((1, hd), lambda i: (0, 0))],
        out_specs=[pl.BlockSpec((tp, n_q * hd), lambda i: (i, 0)),
                   pl.BlockSpec((tp, n_kv * hd), lambda i: (i, 0))],
        out_shape=[jax.ShapeDtypeStruct((r, n_q * hd), BF16), jax.ShapeDtypeStruct((r, n_kv * hd), BF16)],
        compiler_params=_cparams("parallel"),
        name="gqa_qk_prep",
    )(qkv, cos_t, sin_t, q_g.reshape(1, hd), k_g.reshape(1, hd))

    bq = _row_tile(s_lat, r, 256)
    v_col0 = n_q + n_kv
    ctx0 = (n_batch * s_lat) // n_ctx
    nq_b = s_lat // bq
    gw = n_grp * hd

    def flash_scratch(rows_q):
        rows = n_grp * rows_q
        return [pltpu.VMEM((rows, hd), BF16), pltpu.VMEM((rows, 1), F32),
                pltpu.VMEM((rows, 1), F32), pltpu.VMEM((rows, hd), F32)]

    lat_kv = [pl.BlockSpec((s_lat, hd), lambda b, kh, i: (b, kh)),
              pl.BlockSpec((s_lat, hd), lambda b, kh, i: (b, v_col0 + kh)),
              pl.BlockSpec((n_ctx, hd), lambda b, kh, i: (ctx0 + b, kh)),
              pl.BlockSpec((n_ctx, hd), lambda b, kh, i: (ctx0 + b, v_col0 + kh))]
    o_lat = pl.pallas_call(
        functools.partial(_flash_kernel, n_kv_refs=2, n_grp=n_grp, bk_max=512),
        grid=(n_batch, n_kv, nq_b),
        in_specs=[pl.BlockSpec((bq, gw), lambda b, kh, i: (b * nq_b + i, kh))] + lat_kv,
        out_specs=pl.BlockSpec((bq, gw), lambda b, kh, i: (b * nq_b + i, kh)),
        out_shape=jax.ShapeDtypeStruct((n_batch * s_lat, d), BF16),
        scratch_shapes=flash_scratch(bq),
        compiler_params=_cparams("parallel", "parallel", "parallel"),
        name="gqa_flash_lat",
    )(qn, kn, qkv, kn, qkv)
    o_ctx = pl.pallas_call(
        functools.partial(_flash_kernel, n_kv_refs=1, n_grp=n_grp, bk_max=512),
        grid=(n_batch, n_kv),
        in_specs=[pl.BlockSpec((n_ctx, gw), lambda b, kh: (ctx0 + b, kh)),
                  pl.BlockSpec((n_ctx, hd), lambda b, kh: (ctx0 + b, kh)),
                  pl.BlockSpec((n_ctx, hd), lambda b, kh: (ctx0 + b, v_col0 + kh))],
        out_specs=pl.BlockSpec((n_ctx, gw), lambda b, kh: (b, kh)),
        out_shape=jax.ShapeDtypeStruct((n_batch * n_ctx, d), BF16),
        scratch_shapes=flash_scratch(n_ctx),
        compiler_params=_cparams("parallel", "parallel"),
        name="gqa_flash_ctx",
    )(qn, kn, qkv)
    attn = jnp.concatenate([o_lat, o_ctx], axis=0)
    a_specs = [pl.BlockSpec((tm, d), lambda i: (i, 0))]
    return _out_ln([attn], a_specs, lambda a_ref: a_ref[...], w_out.astype(BF16), x, mod, ln_g, ln_b,
                   gate_row=2, alpha=alpha, s_lat=s_lat, n_batch=n_batch, tm=tm, name="gqa_out")


def _top_values(e, k, t_ref):
    work = e
    for i in range(k):
        m = jnp.max(work, axis=0, keepdims=True)
        t_ref[i:i + 1, :] = jnp.maximum(m, 0.0)
        work = jnp.where(work == m, -1.0, work)
    return t_ref[...]


def _pair_products(t1, t2):
    pieces = [t1[0:1] * t2]
    pieces += [t1[a:a + 1] * t2[0:8] for a in range(1, 8)]
    pieces += [t1[8:16] * t2[0:1]]
    return jnp.concatenate(pieces, axis=0)


def _peer_topk_kernel(x_ref, mod_ref, wq_ref, k1_ref, k2_ref, h_ref, a_ref, b_ref, thr_ref,
                      t1_ref, t2_ref, *, dq):
    h = (x_ref[...] * (1.0 + mod_ref[4:5, :]) + mod_ref[3:4, :]).astype(BF16)
    h_ref[...] = h
    q = jnp.dot(h, wq_ref[...], preferred_element_type=F32).astype(BF16)
    half = dq // 2
    for hd in range(P_HEADS):
        q1 = q[:, hd * dq:hd * dq + half]
        q2 = q[:, hd * dq + half:(hd + 1) * dq]
        s1 = lax.dot_general(k1_ref[...], q1, _NT, preferred_element_type=F32)
        s2 = lax.dot_general(k2_ref[...], q2, _NT, preferred_element_type=F32)
        e1 = jnp.exp(s1 - jnp.max(s1, axis=0, keepdims=True))
        e2 = jnp.exp(s2 - jnp.max(s2, axis=0, keepdims=True))
        t1 = _top_values(e1, P_TOPK, t1_ref)
        t2 = _top_values(e2, P_TOPK, t2_ref)
        cand = _pair_products(t1, t2)
        work = cand
        z = jnp.zeros_like(cand[0:1])
        for _ in range(P_TOPK):
            m = jnp.max(work, axis=0, keepdims=True)
            z = z + jnp.maximum(m, 0.0)
            work = jnp.where(work == m, -1.0, work)
        thr_e = jnp.maximum(m, 0.0)
        rz = 1.0 / z
        cand_scaled = _pair_products(t1 * rz, t2)
        thr = jnp.min(jnp.where(cand >= thr_e, cand_scaled, jnp.inf), axis=0, keepdims=True)
        a_ref[hd] = e1 * rz
        b_ref[hd] = e2
        thr_ref[hd:hd + 1, :] = thr


def _peer_dense_kernel(h_ref, u_ref, vt_ref, a_ref, b_ref, thr_ref, x_ref, mod_ref, g_ref, bb_ref,
                       o_ref, acc_ref, at_ref, hs_ref, *, n_i1, alpha):
    e = pl.program_id(1)

    @pl.when(e == 0)
    def _():
        acc_ref[...] = jnp.zeros_like(acc_ref)

    at_ref[...] = lax.dot_general(u_ref[...], h_ref[...], _NT, preferred_element_type=F32)

    def body(j, carry):
        off = pl.multiple_of(j * P_NKEYS, P_NKEYS)
        act = _gelu(at_ref[pl.ds(off, P_NKEYS), :])
        gate = jnp.zeros_like(act)
        for hd in range(P_HEADS):
            p = a_ref[hd, pl.ds(e * n_i1 + j, 1), :] * b_ref[hd]
            gate = gate + jnp.where(p >= thr_ref[hd:hd + 1, :], p, 0.0)
        hs_ref[pl.ds(off, P_NKEYS), :] = (act * gate).astype(BF16)
        return carry

    lax.fori_loop(0, n_i1, body, 0)
    acc_ref[...] += jnp.dot(vt_ref[...], hs_ref[...], preferred_element_type=F32)

    @pl.when(e == pl.num_programs(1) - 1)
    def _():
        f = acc_ref[...].T
        r = alpha * x_ref[...] + mod_ref[5:6, :] * f
        o_ref[...] = _layer_norm_rows(r, g_ref[...], bb_ref[...])


def _peer(x, mod, ln_g, ln_b, w_q, k1, k2, w_u, w_v, *, alpha, s_lat, n_batch):
    r, d = x.shape
    nq = w_q.shape[1]
    dq = nq // P_HEADS
    nk = P_NKEYS
    tk = _row_tile(s_lat, r, 256)
    hmod, a, b, thr = pl.pallas_call(
        functools.partial(_peer_topk_kernel, dq=dq),
        grid=(r // tk,),
        in_specs=[pl.BlockSpec((tk, d), lambda i: (i, 0)),
                  pl.BlockSpec((None, 8, d), _mod_index(tk, s_lat, n_batch)),
                  pl.BlockSpec((d, nq), lambda i: (0, 0)),
                  pl.BlockSpec((nk, dq // 2), lambda i: (0, 0)),
                  pl.BlockSpec((nk, dq // 2), lambda i: (0, 0))],
        out_specs=[pl.BlockSpec((tk, d), lambda i: (i, 0)),
                   pl.BlockSpec((P_HEADS, nk, tk), lambda i: (0, 0, i)),
                   pl.BlockSpec((P_HEADS, nk, tk), lambda i: (0, 0, i)),
                   pl.BlockSpec((P_HEADS, tk), lambda i: (0, i))],
        out_shape=[jax.ShapeDtypeStruct((r, d), BF16),
                   jax.ShapeDtypeStruct((P_HEADS, nk, r), F32),
                   jax.ShapeDtypeStruct((P_HEADS, nk, r), F32),
                   jax.ShapeDtypeStruct((P_HEADS, r), F32)],
        scratch_shapes=[pltpu.VMEM((P_TOPK, tk), F32), pltpu.VMEM((P_TOPK, tk), F32)],
        compiler_params=_cparams("parallel"),
        name="peer_topk",
    )(x, mod, w_q.astype(BF16), k1.astype(BF16), k2.astype(BF16))

    n_exp = w_u.shape[0]
    tm = _row_tile(s_lat, r, 512)
    n_i1 = 8
    te = n_i1 * nk
    once = pl.Buffered(1)
    return pl.pallas_call(
        functools.partial(_peer_dense_kernel, n_i1=n_i1, alpha=alpha),
        grid=(r // tm, n_exp // te),
        in_specs=[pl.BlockSpec((tm, d), lambda i, e: (i, 0), pipeline_mode=once),
                  pl.BlockSpec((te, d), lambda i, e: (e, 0)),
                  pl.BlockSpec((d, te), lambda i, e: (0, e)),
                  pl.BlockSpec((P_HEADS, nk, tm), lambda i, e: (0, 0, i), pipeline_mode=once),
                  pl.BlockSpec((P_HEADS, nk, tm), lambda i, e: (0, 0, i), pipeline_mode=once),
                  pl.BlockSpec((P_HEADS, tm), lambda i, e: (0, i)),
                  pl.BlockSpec((tm, d), lambda i, e: (i, 0), pipeline_mode=once),
                  pl.BlockSpec((None, 8, d), _mod_index(tm, s_lat, n_batch)),
                  pl.BlockSpec((1, d), lambda i, e: (0, 0)),
                  pl.BlockSpec((1, d), lambda i, e: (0, 0))],
        out_specs=pl.BlockSpec((tm, d), lambda i, e: (i, 0)),
        out_shape=jax.ShapeDtypeStruct((r, d), F32),
        scratch_shapes=[pltpu.VMEM((d, tm), F32), pltpu.VMEM((te, tm), F32), pltpu.VMEM((te, tm), BF16)],
        compiler_params=_cparams("parallel", "arbitrary"),
        name="peer_dense",
    )(hmod, w_u.astype(BF16), w_v.T.astype(BF16), a, b, thr, x, mod, ln_g.reshape(1, d), ln_b.reshape(1, d))


def kernel(x, c, ctx, c_ctx, w_mod, b_mod, ln_g, ln_b, a_w_in, a_b_gate, a_norm_g, a_w_out, b_w_in, b_b_in, b_ln_g, b_ln_b, b_w_s, b_b_s, b_w_out, c_w_qkv, c_q_g, c_k_g, c_w_out, p_w_q, p_k1, p_k2, p_u, p_v):
    n_batch, s_lat, d = x.shape
    n_ctx = ctx.shape[1]
    depth = w_mod.shape[0]
    alpha = (2 * depth) ** 0.25
    kw = dict(alpha=alpha, s_lat=s_lat, n_batch=n_batch)

    c_all = jnp.concatenate([c, c_ctx[None, :]], axis=0)
    c_all = jnp.pad(c_all, ((0, (-c_all.shape[0]) % 8), (0, 0)))
    mods = _modulation(c_all, w_mod, b_mod)[:, :n_batch + 1].reshape(depth, n_batch + 1, N_MOD, d)
    mods = jnp.pad(mods, ((0, 0), (0, 0), (0, 8 - N_MOD), (0, 0)))

    xs = jnp.concatenate([x.reshape(n_batch * s_lat, d), ctx.reshape(n_batch * n_ctx, d)], axis=0)
    for i in range(depth):
        mixer, j = i % 3, i // 3
        mod = mods[i]
        if mixer == 0:
            xs = _mlstm_mixer(xs, mod, ln_g[i, 0], ln_b[i, 0], a_w_in[j], a_b_gate[j], a_norm_g[j], a_w_out[j],
                              n_ctx=n_ctx, **kw)
        elif mixer == 1:
            xs = _gmlp_mixer(xs, mod, ln_g[i, 0], ln_b[i, 0], b_w_in[j], b_b_in[j], b_ln_g[j], b_ln_b[j],
                             b_w_s[j], b_b_s[j], b_w_out[j], **kw)
        else:
            xs = _gqa_mixer(xs, mod, ln_g[i, 0], ln_b[i, 0], c_w_qkv[j], c_q_g[j], c_k_g[j], c_w_out[j],
                            n_ctx=n_ctx, **kw)
        xs = _peer(xs, mod, ln_g[i, 1], ln_b[i, 1], p_w_q[i], p_k1[i], p_k2[i], p_u[i], p_v[i], **kw)
    return xs[:n_batch * s_lat].reshape(n_batch, s_lat, d)
```

```python
import functools
import math

import jax
import jax.numpy as jnp
from jax import lax
from jax.experimental import pallas as pl
from jax.experimental.pallas import tpu as pltpu

F32 = jnp.float32
BF16 = jnp.bfloat16

LN_EPS = 1e-6
GRID_W = 64
N_MOD = 6
A_HEADS = 4
A_CHUNK = 128
GATE_CAP = 15.0
B_CHUNK = 128
B_GROUPS = 8
C_HEADS = 16
C_KV_HEADS = 4
C_HEAD_DIM = 128
ROPE_THETA = 10000.0
P_HEADS = 8
P_NKEYS = 128
P_TOPK = 16
PEER_SLAB_KEYS = 4

VMEM_LIMIT = 56 * 1024 * 1024
_NT = (((1,), (1,)), ((), ()))


def _cparams(*sem):
    return pltpu.CompilerParams(dimension_semantics=sem, vmem_limit_bytes=VMEM_LIMIT)


def _row_tile(n_lat_per_batch, n_rows, pref):
    t = pref
    while t > 8 and (n_lat_per_batch % t or n_rows % t):
        t //= 2
    return t


def _gelu(y):
    u = y * (0.7978845608028654 + (0.7978845608028654 * 0.044715) * (y * y))
    hy = 0.5 * y
    return hy + hy * jnp.tanh(u)


def _layer_norm_rows(r, g, b):
    mu = jnp.mean(r, axis=-1, keepdims=True)
    d = r - mu
    var = jnp.mean(d * d, axis=-1, keepdims=True)
    return d * lax.rsqrt(var + LN_EPS) * g + b


def _mod_index(tm, s_lat, n_batch):
    def idx(i, *_):
        return (jnp.minimum((i * tm) // s_lat, n_batch), 0, 0)
    return idx


def _mod_kernel(c_ref, w_ref, b_ref, o_ref):
    c = c_ref[...]
    s = c / (1.0 + jnp.exp(-c))
    o_ref[...] = jnp.dot(s, w_ref[...], preferred_element_type=F32,
                         precision=lax.Precision.HIGHEST) + b_ref[...]


def _modulation(c_all, w_mod, b_mod):
    depth, d, n = w_mod.shape
    rows = c_all.shape[0]
    tn = 1024
    return pl.pallas_call(
        _mod_kernel,
        grid=(depth, n // tn),
        in_specs=[pl.BlockSpec((rows, d), lambda l, j: (0, 0)),
                  pl.BlockSpec((None, d, tn), lambda l, j: (l, 0, j)),
                  pl.BlockSpec((None, 1, tn), lambda l, j: (l, 0, j))],
        out_specs=pl.BlockSpec((None, rows, tn), lambda l, j: (l, 0, j)),
        out_shape=jax.ShapeDtypeStruct((depth, rows, n), F32),
        compiler_params=_cparams("parallel", "parallel"),
        name="adaln_mod",
    )(c_all, w_mod, b_mod.reshape(depth, 1, n))


def _linear_kernel(x_ref, mod_ref, w_ref, b_ref, o_ref, h_ref, *, shift_row, act):
    @pl.when(pl.program_id(1) == 0)
    def _():
        h = x_ref[...] * (1.0 + mod_ref[shift_row + 1:shift_row + 2, :]) + mod_ref[shift_row:shift_row + 1, :]
        h_ref[...] = h.astype(BF16)

    y = jnp.dot(h_ref[...], w_ref[...], preferred_element_type=F32) + b_ref[...]
    if act == "gelu":
        y = _gelu(y)
    o_ref[...] = y.astype(o_ref.dtype)


def _mod_linear(x, mod, w, b, *, shift_row, act, s_lat, n_batch, tm, tn, out_dtype=BF16, name):
    r, d = x.shape
    n = w.shape[1]
    return pl.pallas_call(
        functools.partial(_linear_kernel, shift_row=shift_row, act=act),
        grid=(r // tm, n // tn),
        in_specs=[pl.BlockSpec((tm, d), lambda i, j: (i, 0)),
                  pl.BlockSpec((None, 8, d), _mod_index(tm, s_lat, n_batch)),
                  pl.BlockSpec((d, tn), lambda i, j: (0, j)),
                  pl.BlockSpec((1, tn), lambda i, j: (0, j))],
        out_specs=pl.BlockSpec((tm, tn), lambda i, j: (i, j)),
        out_shape=jax.ShapeDtypeStruct((r, n), out_dtype),
        scratch_shapes=[pltpu.VMEM((tm, d), BF16)],
        compiler_params=_cparams("parallel", "arbitrary"),
        name=name,
    )(x, mod, w, b)


def _out_ln_kernel(*refs, a_fn, n_a, gate_row, alpha):
    a_refs = refs[:n_a]
    w_ref, x_ref, mod_ref, g_ref, b_ref, o_ref = refs[n_a:]
    a = a_fn(*a_refs)
    y = jnp.dot(a, w_ref[...], preferred_element_type=F32)
    r = alpha * x_ref[...] + mod_ref[gate_row:gate_row + 1, :] * y
    o_ref[...] = _layer_norm_rows(r, g_ref[...], b_ref[...])


def _out_ln(a_inputs, a_specs, a_fn, w, x, mod, ln_g, ln_b, *, gate_row, alpha, s_lat, n_batch, tm, name):
    r, d = x.shape
    k = w.shape[0]
    return pl.pallas_call(
        functools.partial(_out_ln_kernel, a_fn=a_fn, n_a=len(a_inputs), gate_row=gate_row, alpha=alpha),
        grid=(r // tm,),
        in_specs=list(a_specs) + [
            pl.BlockSpec((k, d), lambda i: (0, 0)),
            pl.BlockSpec((tm, d), lambda i: (i, 0)),
            pl.BlockSpec((None, 8, d), _mod_index(tm, s_lat, n_batch)),
            pl.BlockSpec((1, d), lambda i: (0, 0)),
            pl.BlockSpec((1, d), lambda i: (0, 0))],
        out_specs=pl.BlockSpec((tm, d), lambda i: (i, 0)),
        out_shape=jax.ShapeDtypeStruct((r, d), F32),
        compiler_params=_cparams("parallel"),
        name=name,
    )(*a_inputs, w, x, mod, ln_g.reshape(1, d), ln_b.reshape(1, d))


def _gate_prep_kernel(x_ref, mod_ref, wi_ref, wf_ref, bi_ref, bf_ref, r_ref, b_ref, *, n_chunks):
    h = x_ref[...] * (1.0 + mod_ref[1:2, :]) + mod_ref[0:1, :]
    hp = lax.Precision.HIGHEST
    gi = jnp.dot(h, wi_ref[...], preferred_element_type=F32, precision=hp) + bi_ref[...]
    gf = jnp.dot(h, wf_ref[...], preferred_element_type=F32, precision=hp) + bf_ref[...]
    gi = GATE_CAP * jnp.tanh(gi / GATE_CAP)
    gf = GATE_CAP * jnp.tanh(gf / GATE_CAP)
    lf = jnp.minimum(gf, 0.0) - jnp.log(1.0 + jnp.exp(-jnp.abs(gf)))
    ti = lax.broadcasted_iota(jnp.int32, (A_CHUNK, A_CHUNK), 0)
    si = lax.broadcasted_iota(jnp.int32, (A_CHUNK, A_CHUNK), 1)
    tri_lo = (si <= ti).astype(F32)
    tri_up = (si >= ti).astype(F32)
    lane = lax.broadcasted_iota(jnp.int32, (A_CHUNK, 128), 1)
    for c in range(n_chunks):
        sl = slice(c * A_CHUNK, (c + 1) * A_CHUNK)
        lfc = lf[sl]
        cum_f = jnp.dot(tri_lo, lfc, preferred_element_type=F32, precision=hp)
        cum_b = jnp.dot(tri_up, lfc, preferred_element_type=F32, precision=hp)
        bcum = jnp.where(lane < A_HEADS, cum_f, cum_b)
        b_ref[sl, :] = bcum
        r_ref[sl, :] = gi[sl] - bcum


def _mlstm_gates(x, mod, w_gate, b_gate, *, s_lat, n_batch, tm):
    r, d = x.shape
    nh = A_HEADS
    wg = w_gate.reshape(d, 4, nh)
    bg = b_gate.reshape(4, nh)
    pad = 128 - 2 * nh
    wi = jnp.pad(jnp.concatenate([wg[:, 0], wg[:, 2]], axis=1), ((0, 0), (0, pad)))
    wf = jnp.pad(jnp.concatenate([wg[:, 1], wg[:, 3]], axis=1), ((0, 0), (0, pad)))
    bi = jnp.pad(jnp.concatenate([bg[0], bg[2]]), (0, pad)).reshape(1, 128)
    bf = jnp.pad(jnp.concatenate([bg[1], bg[3]]), (0, pad)).reshape(1, 128)
    row = pl.BlockSpec((tm, 128), lambda i: (i, 0))
    full = lambda shape: pl.BlockSpec(shape, lambda i: (0, 0))
    return pl.pallas_call(
        functools.partial(_gate_prep_kernel, n_chunks=tm // A_CHUNK),
        grid=(r // tm,),
        in_specs=[pl.BlockSpec((tm, d), lambda i: (i, 0)),
                  pl.BlockSpec((None, 8, d), _mod_index(tm, s_lat, n_batch)),
                  full((d, 128)), full((d, 128)), full((1, 128)), full((1, 128))],
        out_specs=[row, row],
        out_shape=[jax.ShapeDtypeStruct((r, 128), F32)] * 2,
        compiler_params=_cparams("parallel"),
        name="mlstm_gates",
    )(x, mod, wi, wf, bi, bf)


def _mlstm_scan_kernel(q_ref, k_ref, v_ref, rcol_ref, bcol_ref, rrow_ref, o_ref,
                       ct_ref, n_ref, m_ref, *, dk, dv):
    d = pl.program_id(1)
    c = pl.program_id(2)

    @pl.when(c == 0)
    def _():
        ct_ref[...] = jnp.zeros_like(ct_ref)
        n_ref[...] = jnp.zeros_like(n_ref)
        m_ref[...] = jnp.zeros_like(m_ref)

    L = A_CHUNK
    ti = lax.broadcasted_iota(jnp.int32, (L, L), 0)
    si = lax.broadcasted_iota(jnp.int32, (L, L), 1)
    fwd = d == 0
    mask = (si - ti) * (1 - 2 * d) <= 0
    kscale = dk ** -0.5
    for h in range(A_HEADS):
        q = q_ref[:, h * dk:(h + 1) * dk]
        k32 = k_ref[:, h * dk:(h + 1) * dk].astype(F32) * kscale
        k = k32.astype(BF16)
        v = v_ref[:, h * dv:(h + 1) * dv]
        r_col = jnp.where(fwd, rcol_ref[:, h:h + 1], rcol_ref[:, A_HEADS + h:A_HEADS + h + 1])
        b_col = jnp.where(fwd, bcol_ref[:, h:h + 1], bcol_ref[:, A_HEADS + h:A_HEADS + h + 1])
        r_row = rrow_ref[pl.ds(d * A_HEADS + h, 1), :]
        m_prev = m_ref[h:h + 1, 0:1]
        n_row = n_ref[h:h + 1, :]

        dlog = jnp.where(mask, r_row, -jnp.inf)
        big_m = jnp.maximum(jnp.max(dlog, axis=1, keepdims=True), m_prev)
        w_intra = jnp.exp(dlog - big_m)
        w_inter = jnp.exp(m_prev - big_m)
        s = lax.dot_general(q, k, _NT, preferred_element_type=F32) * w_intra
        inter = jnp.dot(q, ct_ref[h].astype(BF16), preferred_element_type=F32)
        num = w_inter * inter + jnp.dot(s.astype(BF16), v, preferred_element_type=F32)
        qn = jnp.sum(q.astype(F32) * n_row, axis=1, keepdims=True)
        den = w_inter * qn + jnp.sum(s, axis=1, keepdims=True)
        hout = num / jnp.maximum(jnp.abs(den), jnp.exp(-(b_col + big_m)))
        o_ref[:, h * dv:(h + 1) * dv] = hout.astype(o_ref.dtype)

        m_last = jnp.maximum(jnp.max(r_col, axis=0, keepdims=True), m_prev)
        g_tot = jnp.min(b_col, axis=0, keepdims=True)
        decay = jnp.exp(m_prev - m_last)
        ws = jnp.exp(r_col - m_last)
        vw = (v.astype(F32) * ws).astype(BF16)
        kt = k32.T.astype(BF16)
        ct_ref[h] = decay * ct_ref[h] + jnp.dot(kt, vw, preferred_element_type=F32)
        n_ref[h:h + 1, :] = decay * n_row + jnp.sum(k32 * ws, axis=0, keepdims=True)
        m_ref[h:h + 1, :] = jnp.broadcast_to(g_tot + m_last, (1, m_ref.shape[1]))


def _mlstm_scan(z, r_col, b_col, r_row, *, n_batch, s_lat, n_ctx, dk, dv):
    r = z.shape[0]
    L = A_CHUNK
    ncl, ncc = s_lat // L, n_ctx // L
    ctx0 = n_batch * ncl

    def chunk(b, d, c):
        pos_c = jnp.where(d == 0, c, ncc - 1 - c)
        pos_l = jnp.where(d == 0, c - ncc, ncl - 1 - (c - ncc))
        return jnp.where(c < ncc, ctx0 + b * ncc + pos_c, b * ncl + pos_l)

    hq = A_HEADS * dk
    hv = A_HEADS * dv
    return pl.pallas_call(
        functools.partial(_mlstm_scan_kernel, dk=dk, dv=dv),
        grid=(n_batch, 2, ncc + ncl),
        in_specs=[pl.BlockSpec((L, hq), lambda b, d, c: (chunk(b, d, c), 0)),
                  pl.BlockSpec((L, hq), lambda b, d, c: (chunk(b, d, c), 1)),
                  pl.BlockSpec((L, hv), lambda b, d, c: (chunk(b, d, c), (2 * hq) // hv)),
                  pl.BlockSpec((L, 128), lambda b, d, c: (chunk(b, d, c), 0)),
                  pl.BlockSpec((L, 128), lambda b, d, c: (chunk(b, d, c), 0)),
                  pl.BlockSpec((2 * A_HEADS, L), lambda b, d, c: (0, chunk(b, d, c)))],
        out_specs=pl.BlockSpec((None, L, hv), lambda b, d, c: (d, chunk(b, d, c), 0)),
        out_shape=jax.ShapeDtypeStruct((2, r, hv), BF16),
        scratch_shapes=[pltpu.VMEM((A_HEADS, dk, dv), F32),
                        pltpu.VMEM((8, dk), F32),
                        pltpu.VMEM((8, 128), F32)],
        compiler_params=_cparams("parallel", "parallel", "arbitrary"),
        name="mlstm_scan",
    )(z, z, z, r_col, b_col, r_row)


def _mlstm_out_a(hf_ref, hb_ref, o_ref, g_ref, *, dv):
    hs = hf_ref[...].astype(F32) + hb_ref[...].astype(F32)
    og = o_ref[...].astype(F32)
    og = 1.0 / (1.0 + jnp.exp(-og))
    parts = []
    for h in range(A_HEADS):
        x = hs[:, h * dv:(h + 1) * dv]
        parts.append(x * lax.rsqrt(jnp.mean(x * x, axis=-1, keepdims=True) + LN_EPS))
    hn = jnp.concatenate(parts, axis=1)
    return (hn * g_ref[...] * og).astype(BF16)


def _mlstm_mixer(x, mod, ln_g, ln_b, w_in, b_gate, norm_g, w_out, *, alpha, s_lat, n_ctx, n_batch):
    r, d = x.shape
    dv = d // A_HEADS
    dk = dv // 2
    n_main = 2 * A_HEADS * dk + A_HEADS * dv + d
    tm = _row_tile(s_lat, r, 512)
    w_main = w_in[:, :n_main].astype(BF16)
    z = _mod_linear(x, mod, w_main, jnp.zeros((1, n_main), F32), shift_row=0, act=None,
                    s_lat=s_lat, n_batch=n_batch, tm=tm, tn=1024, name="mlstm_in")
    r_col, b_col = _mlstm_gates(x, mod, w_in[:, n_main:], b_gate, s_lat=s_lat, n_batch=n_batch, tm=tm)
    r_row = r_col[:, :2 * A_HEADS].T
    hdir = _mlstm_scan(z, r_col, b_col, r_row, n_batch=n_batch, s_lat=s_lat, n_ctx=n_ctx, dk=dk, dv=dv)
    hv = A_HEADS * dv
    a_specs = [pl.BlockSpec((None, tm, hv), lambda i: (0, i, 0)),
               pl.BlockSpec((None, tm, hv), lambda i: (1, i, 0)),
               pl.BlockSpec((tm, d), lambda i: (i, (n_main - d) // d)),
               pl.BlockSpec((1, hv), lambda i: (0, 0))]
    return _out_ln([hdir, hdir, z, norm_g.reshape(1, hv)], a_specs, functools.partial(_mlstm_out_a, dv=dv),
                   w_out.astype(BF16), x, mod, ln_g, ln_b, gate_row=2, alpha=alpha,
                   s_lat=s_lat, n_batch=n_batch, tm=tm, name="mlstm_out")


def _gmlp_gate_kernel(z_ref, g_ref, b_ref, ws_ref, bs_ref, o_ref, *, n_chunks, inner):
    gd = inner // B_GROUPS
    for c in range(n_chunks):
        sl = slice(c * B_CHUNK, (c + 1) * B_CHUNK)
        v = z_ref[sl, inner:].astype(F32)
        vn = _layer_norm_rows(v, g_ref[...], b_ref[...]).astype(BF16)
        for g in range(B_GROUPS):
            cs = slice(g * gd, (g + 1) * gd)
            sv = jnp.dot(ws_ref[g], vn[:, cs], preferred_element_type=F32) + bs_ref[:, g:g + 1]
            o_ref[sl, cs] = (z_ref[sl, cs].astype(F32) * sv).astype(o_ref.dtype)


def _gmlp_mixer(x, mod, ln_g, ln_b, w_in, b_in, g_v, b_v, w_s, b_s, w_out, *, alpha, s_lat, n_batch):
    r, d = x.shape
    inner = w_out.shape[0]
    tm = _row_tile(s_lat, r, 512)
    z = _mod_linear(x, mod, w_in.astype(BF16), b_in.reshape(1, -1), shift_row=0, act="gelu",
                    s_lat=s_lat, n_batch=n_batch, tm=tm, tn=1024, name="gmlp_in")
    tg = _row_tile(s_lat, r, 256)
    gated = pl.pallas_call(
        functools.partial(_gmlp_gate_kernel, n_chunks=tg // B_CHUNK, inner=inner),
        grid=(r // tg,),
        in_specs=[pl.BlockSpec((tg, 2 * inner), lambda i: (i, 0)),
                  pl.BlockSpec((1, inner), lambda i: (0, 0)),
                  pl.BlockSpec((1, inner), lambda i: (0, 0)),
                  pl.BlockSpec((B_GROUPS, B_CHUNK, B_CHUNK), lambda i: (0, 0, 0)),
                  pl.BlockSpec((B_CHUNK, B_GROUPS), lambda i: (0, 0))],
        out_specs=pl.BlockSpec((tg, inner), lambda i: (i, 0)),
        out_shape=jax.ShapeDtypeStruct((r, inner), BF16),
        compiler_params=_cparams("parallel"),
        name="gmlp_gate",
    )(z, g_v.reshape(1, inner), b_v.reshape(1, inner), w_s.astype(BF16), b_s.T)
    a_specs = [pl.BlockSpec((tm, inner), lambda i: (i, 0))]
    return _out_ln([gated], a_specs, lambda a_ref: a_ref[...], w_out.astype(BF16), x, mod, ln_g, ln_b,
                   gate_row=2, alpha=alpha, s_lat=s_lat, n_batch=n_batch, tm=tm, name="gmlp_out")


def _qk_prep_kernel(qkv_ref, cos_ref, sin_ref, qg_ref, kg_ref, q_ref, k_ref, *, n_q, n_kv, scale):
    hd = C_HEAD_DIM
    cos = cos_ref[...]
    sin = sin_ref[...]
    lane = lax.broadcasted_iota(jnp.int32, cos.shape, 1)
    first = (lane % (hd // 2)) < (hd // 4)

    def norm_rope(x, g):
        xn = x * lax.rsqrt(jnp.mean(x * x, axis=-1, keepdims=True) + LN_EPS) * g
        swapped = jnp.where(first, pltpu.roll(xn, hd - hd // 4, axis=1), pltpu.roll(xn, hd // 4, axis=1))
        return xn * cos + swapped * sin

    for h in range(n_q):
        x = qkv_ref[:, h * hd:(h + 1) * hd].astype(F32)
        q_ref[:, h * hd:(h + 1) * hd] = (norm_rope(x, qg_ref[...]) * scale).astype(q_ref.dtype)
    for h in range(n_kv):
        x = qkv_ref[:, (n_q + h) * hd:(n_q + h + 1) * hd].astype(F32)
        k_ref[:, h * hd:(h + 1) * hd] = norm_rope(x, kg_ref[...]).astype(k_ref.dtype)


def _flash_kernel(q_ref, *refs, n_kv_refs, n_grp, bk_max):
    kv_refs = refs[:2 * n_kv_refs]
    o_ref, qs_ref, m_ref, l_ref, acc_ref = refs[2 * n_kv_refs:]
    hd = C_HEAD_DIM
    bq = q_ref.shape[0]
    for g in range(n_grp):
        qs_ref[g * bq:(g + 1) * bq, :] = q_ref[:, g * hd:(g + 1) * hd]
    m_ref[...] = jnp.full_like(m_ref, -jnp.inf)
    l_ref[...] = jnp.zeros_like(l_ref)
    acc_ref[...] = jnp.zeros_like(acc_ref)

    def update(kb, vb):
        s = lax.dot_general(qs_ref[...], kb, _NT, preferred_element_type=F32)
        m = m_ref[...]
        m_new = jnp.maximum(m, jnp.max(s, axis=1, keepdims=True))
        p = jnp.concatenate([jnp.exp(s[:, c * hd:(c + 1) * hd] - m_new) for c in range(s.shape[1] // hd)], axis=1)
        a = jnp.exp(m - m_new)
        l_ref[...] = a * l_ref[...] + jnp.sum(p, axis=1, keepdims=True)
        acc_ref[...] = a * acc_ref[...] + jnp.dot(p.astype(BF16), vb, preferred_element_type=F32)
        m_ref[...] = m_new

    for i in range(n_kv_refs):
        k_ref, v_ref = kv_refs[2 * i], kv_refs[2 * i + 1]
        bk = min(bk_max, k_ref.shape[0])
        n_blk = k_ref.shape[0] // bk
        if n_blk == 1:
            update(k_ref[...], v_ref[...])
        else:
            def body(j, c, k_ref=k_ref, v_ref=v_ref):
                off = pl.multiple_of(j * bk, bk)
                update(k_ref[pl.ds(off, bk), :], v_ref[pl.ds(off, bk), :])
                return c
            lax.fori_loop(0, n_blk, body, 0)
    out = acc_ref[...] / l_ref[...]
    for g in range(n_grp):
        o_ref[:, g * hd:(g + 1) * hd] = out[g * bq:(g + 1) * bq].astype(o_ref.dtype)


def _gqa_mixer(x, mod, ln_g, ln_b, w_qkv, q_g, k_g, w_out, *, alpha, s_lat, n_ctx, n_batch):
    r, d = x.shape
    hd = C_HEAD_DIM
    n_q, n_kv = C_HEADS, C_KV_HEADS
    n_grp = n_q // n_kv
    n_qkv = (n_q + 2 * n_kv) * hd
    tm = _row_tile(s_lat, r, 512)
    qkv = _mod_linear(x, mod, w_qkv.astype(BF16), jnp.zeros((1, n_qkv), F32), shift_row=0, act=None,
                      s_lat=s_lat, n_batch=n_batch, tm=tm, tn=1024, name="gqa_qkv")

    tp = _row_tile(s_lat, r, 256)
    pos = jnp.arange(s_lat, dtype=jnp.int32)
    quarter = hd // 4
    freqs = ROPE_THETA ** (-jnp.arange(quarter, dtype=F32) / quarter)
    ang_r = (pos // GRID_W).astype(F32)[:, None] * freqs
    ang_c = (pos % GRID_W).astype(F32)[:, None] * freqs
    cos_t = jnp.concatenate([jnp.cos(ang_r)] * 2 + [jnp.cos(ang_c)] * 2, axis=1)
    sin_t = jnp.concatenate([-jnp.sin(ang_r), jnp.sin(ang_r), -jnp.sin(ang_c), jnp.sin(ang_c)], axis=1)
    cos_t = jnp.concatenate([cos_t, jnp.ones((tp, hd), F32)], axis=0)
    sin_t = jnp.concatenate([sin_t, jnp.zeros((tp, hd), F32)], axis=0)
    n_lat_tiles = (n_batch * s_lat) // tp
    per_b = s_lat // tp
    tab = lambda i: (jnp.where(i < n_lat_tiles, i % per_b, per_b), 0)
    qn, kn = pl.pallas_call(
        functools.partial(_qk_prep_kernel, n_q=n_q, n_kv=n_kv, scale=hd ** -0.5),
        grid=(r // tp,),
        in_specs=[pl.BlockSpec((tp, (n_q + n_kv) * hd), lambda i: (i, 0)),
                  pl.BlockSpec((tp, hd), tab), pl.BlockSpec((tp, hd), tab),
                  pl.BlockSpec((1, hd), lambda i: (0, 0)), pl.BlockSpec((1, hd), lambda i: (0, 0))],
        out_specs=[pl.BlockSpec((tp, n_q * hd), lambda i: (i, 0)),
                   pl.BlockSpec((tp, n_kv * hd), lambda i: (i, 0))],
        out_shape=[jax.ShapeDtypeStruct((r, n_q * hd), BF16), jax.ShapeDtypeStruct((r, n_kv * hd), BF16)],
        compiler_params=_cparams("parallel"),
        name="gqa_qk_prep",
    )(qkv, cos_t, sin_t, q_g.reshape(1, hd), k_g.reshape(1, hd))

    bq = _row_tile(s_lat, r, 256)
    v_col0 = n_q + n_kv
    ctx0 = (n_batch * s_lat) // n_ctx
    nq_b = s_lat // bq
    gw = n_grp * hd

    def flash_scratch(rows_q):
        rows = n_grp * rows_q
        return [pltpu.VMEM((rows, hd), BF16), pltpu.VMEM((rows, hd), F32),
                pltpu.VMEM((rows, hd), F32), pltpu.VMEM((rows, hd), F32)]

    lat_kv = [pl.BlockSpec((s_lat, hd), lambda b, kh, i: (b, kh)),
              pl.BlockSpec((s_lat, hd), lambda b, kh, i: (b, v_col0 + kh)),
              pl.BlockSpec((n_ctx, hd), lambda b, kh, i: (ctx0 + b, kh)),
              pl.BlockSpec((n_ctx, hd), lambda b, kh, i: (ctx0 + b, v_col0 + kh))]
    o_lat = pl.pallas_call(
        functools.partial(_flash_kernel, n_kv_refs=2, n_grp=n_grp, bk_max=512),
        grid=(n_batch, n_kv, nq_b),
        in_specs=[pl.BlockSpec((bq, gw), lambda b, kh, i: (b * nq_b + i, kh))] + lat_kv,
        out_specs=pl.BlockSpec((bq, gw), lambda b, kh, i: (b * nq_b + i, kh)),
        out_shape=jax.ShapeDtypeStruct((n_batch * s_lat, d), BF16),
        scratch_shapes=flash_scratch(bq),
        compiler_params=_cparams("parallel", "parallel", "parallel"),
        name="gqa_flash_lat",
    )(qn, kn, qkv, kn, qkv)
    o_ctx = pl.pallas_call(
        functools.partial(_flash_kernel, n_kv_refs=1, n_grp=n_grp, bk_max=512),
        grid=(n_batch, n_kv),
        in_specs=[pl.BlockSpec((n_ctx, gw), lambda b, kh: (ctx0 + b, kh)),
                  pl.BlockSpec((n_ctx, hd), lambda b, kh: (ctx0 + b, kh)),
                  pl.BlockSpec((n_ctx, hd), lambda b, kh: (ctx0 + b, v_col0 + kh))],
        out_specs=pl.BlockSpec((n_ctx, gw), lambda b, kh: (b, kh)),
        out_shape=jax.ShapeDtypeStruct((n_batch * n_ctx, d), BF16),
        scratch_shapes=flash_scratch(n_ctx),
        compiler_params=_cparams("parallel", "parallel"),
        name="gqa_flash_ctx",
    )(qn, kn, qkv)
    attn = jnp.concatenate([o_lat, o_ctx], axis=0)
    a_specs = [pl.BlockSpec((tm, d), lambda i: (i, 0))]
    return _out_ln([attn], a_specs, lambda a_ref: a_ref[...], w_out.astype(BF16), x, mod, ln_g, ln_b,
                   gate_row=2, alpha=alpha, s_lat=s_lat, n_batch=n_batch, tm=tm, name="gqa_out")


def _top_values(e, k, t_ref):
    work = e
    for i in range(k):
        m = jnp.max(work, axis=0, keepdims=True)
        t_ref[i:i + 1, :] = jnp.maximum(m, 0.0)
        work = jnp.where(work == m, -1.0, work)
    return t_ref[...]


def _pair_products(t1, t2):
    pieces = [t1[0:1] * t2]
    pieces += [t1[a:a + 1] * t2[0:8] for a in range(1, 8)]
    pieces += [t1[8:16] * t2[0:1]]
    return jnp.concatenate(pieces, axis=0)


def _peer_topk_kernel(x_ref, mod_ref, wq_ref, k1_ref, k2_ref, h_ref, a_ref, b_ref, thr_ref,
                      t1_ref, t2_ref, *, dq):
    h = (x_ref[...] * (1.0 + mod_ref[4:5, :]) + mod_ref[3:4, :]).astype(BF16)
    h_ref[...] = h
    q = jnp.dot(h, wq_ref[...], preferred_element_type=F32).astype(BF16)
    half = dq // 2
    for hd in range(P_HEADS):
        q1 = q[:, hd * dq:hd * dq + half]
        q2 = q[:, hd * dq + half:(hd + 1) * dq]
        s1 = lax.dot_general(k1_ref[...], q1, _NT, preferred_element_type=F32)
        s2 = lax.dot_general(k2_ref[...], q2, _NT, preferred_element_type=F32)
        e1 = jnp.exp(s1 - jnp.max(s1, axis=0, keepdims=True))
        e2 = jnp.exp(s2 - jnp.max(s2, axis=0, keepdims=True))
        t1 = _top_values(e1, P_TOPK, t1_ref)
        t2 = _top_values(e2, P_TOPK, t2_ref)
        cand = _pair_products(t1, t2)
        work = cand
        z = jnp.zeros_like(cand[0:1])
        for _ in range(P_TOPK):
            m = jnp.max(work, axis=0, keepdims=True)
            z = z + jnp.maximum(m, 0.0)
            work = jnp.where(work == m, -1.0, work)
        thr_e = jnp.maximum(m, 0.0)
        rz = 1.0 / z
        cand_scaled = _pair_products(t1 * rz, t2)
        thr = jnp.min(jnp.where(cand >= thr_e, cand_scaled, jnp.inf), axis=0, keepdims=True)
        a_ref[hd] = e1 * rz
        b_ref[hd] = e2
        thr_ref[hd:hd + 1, :] = thr


def _peer_dense_kernel(h_ref, u_ref, vt_ref, a_ref, b_ref, thr_ref, x_ref, mod_ref, g_ref, bb_ref,
                       o_ref, acc_ref, *, n_i1, alpha):
    e = pl.program_id(1)

    @pl.when(e == 0)
    def _():
        acc_ref[...] = jnp.zeros_like(acc_ref)

    h = h_ref[...]
    k_slab = PEER_SLAB_KEYS
    slab = k_slab * P_NKEYS
    n_slab = n_i1 // k_slab

    def expert_act(j):
        return lax.dot_general(u_ref[j * slab:(j + 1) * slab, :], h, _NT, preferred_element_type=F32)

    def dense_gate(j):
        gates = []
        for i in range(k_slab):
            gate = None
            for hd in range(P_HEADS):
                p = a_ref[hd, pl.ds(e * n_i1 + k_slab * j + i, 1), :] * b_ref[hd]
                sel = jnp.where(p >= thr_ref[hd:hd + 1, :], p, 0.0)
                gate = sel if gate is None else gate + sel
            gates.append(gate)
        return jnp.concatenate(gates, axis=0)

    act = expert_act(0)
    for j in range(n_slab):
        gate = dense_gate(j)
        act_next = expert_act(j + 1) if j + 1 < n_slab else None
        hs = (_gelu(act) * gate).astype(BF16)
        acc_ref[...] += jnp.dot(vt_ref[:, j * slab:(j + 1) * slab], hs, preferred_element_type=F32)
        act = act_next

    @pl.when(e == pl.num_programs(1) - 1)
    def _():
        f = acc_ref[...].T
        r = alpha * x_ref[...] + mod_ref[5:6, :] * f
        o_ref[...] = _layer_norm_rows(r, g_ref[...], bb_ref[...])


def _peer(x, mod, ln_g, ln_b, w_q, k1, k2, w_u, w_v, *, alpha, s_lat, n_batch):
    r, d = x.shape
    nq = w_q.shape[1]
    dq = nq // P_HEADS
    nk = P_NKEYS
    tk = _row_tile(s_lat, r, 256)
    hmod, a, b, thr = pl.pallas_call(
        functools.partial(_peer_topk_kernel, dq=dq),
        grid=(r // tk,),
        in_specs=[pl.BlockSpec((tk, d), lambda i: (i, 0)),
                  pl.BlockSpec((None, 8, d), _mod_index(tk, s_lat, n_batch)),
                  pl.BlockSpec((d, nq), lambda i: (0, 0)),
                  pl.BlockSpec((nk, dq // 2), lambda i: (0, 0)),
                  pl.BlockSpec((nk, dq // 2), lambda i: (0, 0))],
        out_specs=[pl.BlockSpec((tk, d), lambda i: (i, 0)),
                   pl.BlockSpec((P_HEADS, nk, tk), lambda i: (0, 0, i)),
                   pl.BlockSpec((P_HEADS, nk, tk), lambda i: (0, 0, i)),
                   pl.BlockSpec((P_HEADS, tk), lambda i: (0, i))],
        out_shape=[jax.ShapeDtypeStruct((r, d), BF16),
                   jax.ShapeDtypeStruct((P_HEADS, nk, r), F32),
                   jax.ShapeDtypeStruct((P_HEADS, nk, r), F32),
                   jax.ShapeDtypeStruct((P_HEADS, r), F32)],
        scratch_shapes=[pltpu.VMEM((P_TOPK, tk), F32), pltpu.VMEM((P_TOPK, tk), F32)],
        compiler_params=_cparams("parallel"),
        name="peer_topk",
    )(x, mod, w_q.astype(BF16), k1.astype(BF16), k2.astype(BF16))

    n_exp = w_u.shape[0]
    tm = _row_tile(s_lat, r, 512)
    n_i1 = 8
    te = n_i1 * nk
    once = pl.Buffered(1)
    return pl.pallas_call(
        functools.partial(_peer_dense_kernel, n_i1=n_i1, alpha=alpha),
        grid=(r // tm, n_exp // te),
        in_specs=[pl.BlockSpec((tm, d), lambda i, e: (i, 0), pipeline_mode=once),
                  pl.BlockSpec((te, d), lambda i, e: (e, 0)),
                  pl.BlockSpec((d, te), lambda i, e: (0, e)),
                  pl.BlockSpec((P_HEADS, nk, tm), lambda i, e: (0, 0, i), pipeline_mode=once),
                  pl.BlockSpec((P_HEADS, nk, tm), lambda i, e: (0, 0, i), pipeline_mode=once),
                  pl.BlockSpec((P_HEADS, tm), lambda i, e: (0, i)),
                  pl.BlockSpec((tm, d), lambda i, e: (i, 0), pipeline_mode=once),
                  pl.BlockSpec((None, 8, d), _mod_index(tm, s_lat, n_batch)),
                  pl.BlockSpec((1, d), lambda i, e: (0, 0)),
                  pl.BlockSpec((1, d), lambda i, e: (0, 0))],
        out_specs=pl.BlockSpec((tm, d), lambda i, e: (i, 0)),
        out_shape=jax.ShapeDtypeStruct((r, d), F32),
        scratch_shapes=[pltpu.VMEM((d, tm), F32)],
        compiler_params=_cparams("parallel", "arbitrary"),
        name="peer_dense",
    )(hmod, w_u.astype(BF16), w_v.T.astype(BF16), a, b, thr, x, mod, ln_g.reshape(1, d), ln_b.reshape(1, d))


def kernel(x, c, ctx, c_ctx, w_mod, b_mod, ln_g, ln_b, a_w_in, a_b_gate, a_norm_g, a_w_out, b_w_in, b_b_in, b_ln_g, b_ln_b, b_w_s, b_b_s, b_w_out, c_w_qkv, c_q_g, c_k_g, c_w_out, p_w_q, p_k1, p_k2, p_u, p_v):
    n_batch, s_lat, d = x.shape
    n_ctx = ctx.shape[1]
    depth = w_mod.shape[0]
    alpha = (2 * depth) ** 0.25
    kw = dict(alpha=alpha, s_lat=s_lat, n_batch=n_batch)

    c_all = jnp.concatenate([c, c_ctx[None, :]], axis=0)
    c_all = jnp.pad(c_all, ((0, (-c_all.shape[0]) % 8), (0, 0)))
    mods = _modulation(c_all, w_mod, b_mod)[:, :n_batch + 1].reshape(depth, n_batch + 1, N_MOD, d)
    mods = jnp.pad(mods, ((0, 0), (0, 0), (0, 8 - N_MOD), (0, 0)))

    xs = jnp.concatenate([x.reshape(n_batch * s_lat, d), ctx.reshape(n_batch * n_ctx, d)], axis=0)
    for i in range(depth):
        mixer, j = i % 3, i // 3
        mod = mods[i]
        if mixer == 0:
            xs = _mlstm_mixer(xs, mod, ln_g[i, 0], ln_b[i, 0], a_w_in[j], a_b_gate[j], a_norm_g[j], a_w_out[j],
                              n_ctx=n_ctx, **kw)
        elif mixer == 1:
            xs = _gmlp_mixer(xs, mod, ln_g[i, 0], ln_b[i, 0], b_w_in[j], b_b_in[j], b_ln_g[j], b_ln_b[j],
                             b_w_s[j], b_b_s[j], b_w_out[j], **kw)
        else:
            xs = _gqa_mixer(xs, mod, ln_g[i, 0], ln_b[i, 0], c_w_qkv[j], c_q_g[j], c_k_g[j], c_w_out[j],
                            n_ctx=n_ctx, **kw)
        xs = _peer(xs, mod, ln_g[i, 1], ln_b[i, 1], p_w_q[i], p_k1[i], p_k2[i], p_u[i], p_v[i], **kw)
    return xs[:n_batch * s_lat].reshape(n_batch, s_lat, d)
```
